```python
import math
import jax, jax.numpy as jnp
from jax import lax
import numpy as np

D_MODEL = 1024
BATCH = 4
SEQ = 8192
DEPTH = 4

D_FF = 2816
D_POOL = 256
POOL_WINDOWS = (2, 4, 8, 16)
POOL_GROUP = D_POOL // len(POOL_WINDOWS)
D_CONV = 256
CONV_WIDTH = 3
NA_HEADS = 8
NA_HEAD_DIM = 64
D_NA = NA_HEADS * NA_HEAD_DIM
D_MIX = D_POOL + D_CONV + D_NA
GRID_W = 64
NA_ROWS = 8
NA_COLS = 16
D_IN = D_POOL + 3 * D_CONV + 3 * D_NA
ALPHA = (2.0 * DEPTH) ** 0.25
BETA = (8.0 * DEPTH) ** -0.25
LN_EPS = 1e-5
NEG_INF = -1e30

kernel_name = "hybrid_pool_conv_natten_encoder"


def layer_norm(x, g, b):
    xf = x.astype(jnp.float32)
    mu = jnp.mean(xf, axis=-1, keepdims=True)
    var = jnp.mean(jnp.square(xf - mu), axis=-1, keepdims=True)
    y = (xf - mu) * lax.rsqrt(var + LN_EPS)
    return (y * g.astype(jnp.float32) + b.astype(jnp.float32)).astype(x.dtype)


def swiglu(x, w_gate, w_up, w_down):
    return (jax.nn.silu(x @ w_gate) * (x @ w_up)) @ w_down


def pool_mixer(u, pool_w, pool_scale):
    bsz, s, _ = u.shape
    ng = len(POOL_WINDOWS)
    uf = u.astype(jnp.float32).reshape(bsz, s, ng, POOL_GROUP)
    cs = jnp.concatenate([jnp.zeros((bsz, 1, ng, POOL_GROUP), jnp.float32),
                          jnp.cumsum(uf, axis=1)], axis=1)
    t = jnp.arange(s)
    outs = []
    for g, w in enumerate(POOL_WINDOWS):
        lo = jnp.clip(t - w // 2, 0, s)
        hi = jnp.clip(t - w // 2 + w, 0, s)
        cnt = (hi - lo).astype(jnp.float32)[None, :, None]
        mean = (cs[:, hi, g] - cs[:, lo, g]) / cnt
        outs.append(mean - uf[:, :, g])
    p = jnp.stack(outs, axis=2).astype(u.dtype)
    y = jnp.einsum('bsgc,gcd->bsgd', p, pool_w) * pool_scale.reshape(ng, POOL_GROUP)
    return y.reshape(bsz, s, D_POOL)


def gated_conv_mixer(gate_b, gate_c, h, conv_w):
    z = gate_c * h
    zp = jnp.pad(z, ((0, 0), (1, 1), (0, 0)))
    y = conv_w[0] * zp[:, :-2] + conv_w[1] * zp[:, 1:-1] + conv_w[2] * zp[:, 2:]
    return gate_b * y


def neighbourhood_attention(q, k, v, rpb):
    bsz, s, _ = q.shape
    rows = s // GRID_W
    kr = min(NA_ROWS, rows)
    shp = (bsz, rows, GRID_W, NA_HEADS, NA_HEAD_DIM)
    q, k, v = q.reshape(shp), k.reshape(shp), v.reshape(shp)
    r = jnp.arange(rows)
    row_start = jnp.clip(r - kr // 2, 0, rows - kr)
    row_idx = row_start[:, None] + jnp.arange(kr)[None, :]
    kb = k[:, row_idx]
    vb = v[:, row_idx]
    c = jnp.arange(GRID_W)
    col_start = jnp.clip(c - NA_COLS // 2, 0, GRID_W - NA_COLS)
    col_valid = (c[None, :] >= col_start[:, None]) & (c[None, :] < col_start[:, None] + NA_COLS)
    dr = row_idx - r[:, None] + (NA_ROWS - 1)
    dc = jnp.clip(c[None, :] - c[:, None], -(NA_COLS - 1), NA_COLS - 1) + (NA_COLS - 1)
    bias = rpb[:, dr[:, None, :, None], dc[None, :, None, :]]
    scores = jnp.einsum('brqhd,brikhd->bhrqik', q, kb).astype(jnp.float32) * (NA_HEAD_DIM ** -0.5)
    scores = scores + bias.astype(jnp.float32)
    scores = jnp.where(col_valid[:, None, :], scores, NEG_INF)
    p = jax.nn.softmax(scores, axis=(-2, -1)).astype(v.dtype)
    o = jnp.einsum('bhrqik,brikhd->brqhd', p, vb)
    return o.reshape(bsz, s, D_NA)


def setup_inputs(seed: int = 0) -> dict:
    key = jax.random.key(seed)
    ks = jax.random.split(key, 16)
    L, D, F = DEPTH, D_MODEL, D_FF
    nrm = lambda k, shp: jax.random.normal(k, shp, jnp.float32)
    x = nrm(ks[0], (BATCH, SEQ, D))
    ffn1_w_gate = nrm(ks[1], (L, D, F)) * D ** -0.5
    ffn1_w_up = nrm(ks[2], (L, D, F)) * D ** -0.5
    ffn1_w_down = nrm(ks[3], (L, F, D)) * (BETA * F ** -0.5)
    ffn2_w_gate = nrm(ks[4], (L, D, F)) * D ** -0.5
    ffn2_w_up = nrm(ks[5], (L, D, F)) * D ** -0.5
    ffn2_w_down = nrm(ks[6], (L, F, D)) * (BETA * F ** -0.5)
    col_scale = jnp.concatenate([
        jnp.ones((D_POOL + 2 * D_CONV,), jnp.float32),
        jnp.full((D_CONV,), BETA, jnp.float32),
        jnp.ones((2 * D_NA,), jnp.float32),
        jnp.full((D_NA,), BETA, jnp.float32)])
    w_in = nrm(ks[7], (L, D, D_IN)) * D ** -0.5 * col_scale
    pool_w = nrm(ks[8], (L, len(POOL_WINDOWS), POOL_GROUP, POOL_GROUP)) * POOL_GROUP ** -0.5
    pool_scale = 1.0 + 0.1 * nrm(ks[9], (L, D_POOL))
    conv_w = nrm(ks[10], (L, CONV_WIDTH, D_CONV)) * CONV_WIDTH ** -0.5
    rpb = 0.02 * nrm(ks[11], (L, NA_HEADS, 2 * NA_ROWS - 1, 2 * NA_COLS - 1))
    w_out = nrm(ks[12], (L, D_MIX, D)) * (BETA * D_MIX ** -0.5)
    ln_g = 1.0 + 0.05 * nrm(ks[13], (L, 3, D))
    ln_b = 0.02 * nrm(ks[14], (L, 3, D))
    return {"x": x,
            "ffn1_w_gate": ffn1_w_gate, "ffn1_w_up": ffn1_w_up, "ffn1_w_down": ffn1_w_down,
            "ffn2_w_gate": ffn2_w_gate, "ffn2_w_up": ffn2_w_up, "ffn2_w_down": ffn2_w_down,
            "w_in": w_in, "pool_w": pool_w, "pool_scale": pool_scale, "conv_w": conv_w,
            "rpb": rpb, "w_out": w_out, "ln_g": ln_g, "ln_b": ln_b}


def reference(x, ffn1_w_gate, ffn1_w_up, ffn1_w_down, ffn2_w_gate, ffn2_w_up, ffn2_w_down,
              w_in, pool_w, pool_scale, conv_w, rpb, w_out, ln_g, ln_b):
    splits = np.cumsum([D_POOL, D_CONV, D_CONV, D_CONV, D_NA, D_NA])
    for l in range(DEPTH):
        x = layer_norm(ALPHA * x + 0.5 * swiglu(x, ffn1_w_gate[l], ffn1_w_up[l], ffn1_w_down[l]),
                       ln_g[l, 0], ln_b[l, 0])
        proj = x @ w_in[l]
        u, gb, gc, h, q, k, v = jnp.split(proj, splits, axis=-1)
        y_a = pool_mixer(u, pool_w[l], pool_scale[l])
        y_b = gated_conv_mixer(gb, gc, h, conv_w[l])
        y_c = neighbourhood_attention(q, k, v, rpb[l])
        y = jnp.concatenate([y_a, y_b, y_c], axis=-1) @ w_out[l]
        x = layer_norm(ALPHA * x + y, ln_g[l, 1], ln_b[l, 1])
        x = layer_norm(ALPHA * x + 0.5 * swiglu(x, ffn2_w_gate[l], ffn2_w_up[l], ffn2_w_down[l]),
                       ln_g[l, 2], ln_b[l, 2])
    return x
```

```python
import functools

import numpy as np
import jax
import jax.numpy as jnp
from jax import lax
from jax.experimental import pallas as pl
from jax.experimental.pallas import tpu as pltpu

D_MODEL = 1024
D_FF = 2816
D_POOL = 256
POOL_WINDOWS = (2, 4, 8, 16)
POOL_GROUP = D_POOL // len(POOL_WINDOWS)
D_CONV = 256
NA_HEADS = 8
NA_HEAD_DIM = 64
D_NA = NA_HEADS * NA_HEAD_DIM
D_MIX = D_POOL + D_CONV + D_NA
GRID_W = 64
NA_ROWS = 8
NA_COLS = 16
D_IN = D_POOL + 3 * D_CONV + 3 * D_NA
DEPTH = 4
ALPHA = (2.0 * DEPTH) ** 0.25
LN_EPS = 1e-5
NEG_INF = -1e30

V7X_VMEM_BYTES = 64 * 1024 * 1024
LANES = 128
SUBLANES_F32 = 8
SUBLANES_BF16 = 16
MXU_DIM = 256

BF16 = jnp.bfloat16
F32 = jnp.float32

FFN_TM = 512
FFN_TF = D_FF // 2
PROJ_TM = 512
ROWS_PER_BLOCK = NA_ROWS
MIX_TM = ROWS_PER_BLOCK * GRID_W
KEY_ROWS = 2 * NA_ROWS
HALO = SUBLANES_F32
HEADS_PER_GROUP = MXU_DIM // NA_HEAD_DIM
KEY_COLS = 2 * NA_COLS
COL_BLOCKS = ((0, 24, 0), (24, 16, 16), (40, 24, 32))
VMEM_LIMIT = 56 * 1024 * 1024

assert FFN_TF % LANES == 0 and D_FF % FFN_TF == 0
assert sum(w for _, w, _ in COL_BLOCKS) == GRID_W
assert max(POOL_WINDOWS) // 2 <= HALO


def _layer_norm(y, g, b):
    mu = jnp.mean(y, axis=-1, keepdims=True)
    yc = y - mu
    var = jnp.mean(yc * yc, axis=-1, keepdims=True)
    return yc * lax.rsqrt(var + LN_EPS) * g + b


def _ffn_kernel(x_ref, wg_ref, wu_ref, wd_ref, g_ref, b_ref, o_ref, acc_ref, xb_ref):
    f = pl.program_id(1)

    @pl.when(f == 0)
    def _():
        xb_ref[...] = x_ref[...].astype(BF16)
        acc_ref[...] = jnp.zeros_like(acc_ref)

    xb = xb_ref[...]
    gate = jnp.dot(xb, wg_ref[...], preferred_element_type=F32)
    up = jnp.dot(xb, wu_ref[...], preferred_element_type=F32)
    hid = (gate * jax.nn.sigmoid(gate)) * up
    acc_ref[...] += jnp.dot(hid.astype(BF16), wd_ref[...], preferred_element_type=F32)

    @pl.when(f == pl.num_programs(1) - 1)
    def _():
        y = ALPHA * x_ref[...] + 0.5 * acc_ref[...]
        o_ref[...] = _layer_norm(y, g_ref[...], b_ref[...])


def _ffn(x, wg, wu, wd, ln_g, ln_b, layer, ln_idx):
    t = x.shape[0]
    assert t % FFN_TM == 0
    grid = (t // FFN_TM, D_FF // FFN_TF)
    ln_row = layer * 3 + ln_idx
    return pl.pallas_call(
        _ffn_kernel,
        grid=grid,
        in_specs=[
            pl.BlockSpec((FFN_TM, D_MODEL), lambda i, f: (i, 0)),
            pl.BlockSpec((None, D_MODEL, FFN_TF), lambda i, f: (layer, 0, f)),
            pl.BlockSpec((None, D_MODEL, FFN_TF), lambda i, f: (layer, 0, f)),
            pl.BlockSpec((None, FFN_TF, D_MODEL), lambda i, f: (layer, f, 0)),
            pl.BlockSpec((None, 1, D_MODEL), lambda i, f: (ln_row, 0, 0)),
            pl.BlockSpec((None, 1, D_MODEL), lambda i, f: (ln_row, 0, 0)),
        ],
        out_specs=pl.BlockSpec((FFN_TM, D_MODEL), lambda i, f: (i, 0)),
        out_shape=jax.ShapeDtypeStruct((t, D_MODEL), F32),
        scratch_shapes=[pltpu.VMEM((FFN_TM, D_MODEL), F32),
                        pltpu.VMEM((FFN_TM, D_MODEL), BF16)],
        compiler_params=pltpu.CompilerParams(
            dimension_semantics=("parallel", "arbitrary"),
            vmem_limit_bytes=VMEM_LIMIT),
        name="ffn",
    )(x, wg, wu, wd, ln_g, ln_b)


def _in_proj_kernel(x_ref, w_ref, pa_ref, q_ref, k_ref, v_ref):
    xb = x_ref[...].astype(BF16)
    n_a = D_POOL + 3 * D_CONV
    pa_ref[...] = jnp.dot(xb, w_ref[:, 0:n_a], preferred_element_type=F32)
    q_ref[...] = jnp.dot(xb, w_ref[:, n_a:n_a + D_NA],
                         preferred_element_type=F32) * (NA_HEAD_DIM ** -0.5)
    k_ref[...] = jnp.dot(xb, w_ref[:, n_a + D_NA:n_a + 2 * D_NA],
                         preferred_element_type=F32).astype(BF16)
    v_ref[...] = jnp.dot(xb, w_ref[:, n_a + 2 * D_NA:n_a + 3 * D_NA],
                         preferred_element_type=F32).astype(BF16)


def _in_proj(x, w_in, layer):
    t = x.shape[0]
    assert t % PROJ_TM == 0
    n_a = D_POOL + 3 * D_CONV
    row = lambda i: (i, 0)
    return pl.pallas_call(
        _in_proj_kernel,
        grid=(t // PROJ_TM,),
        in_specs=[
            pl.BlockSpec((PROJ_TM, D_MODEL), row),
            pl.BlockSpec((None, D_MODEL, D_IN), lambda i: (layer, 0, 0)),
        ],
        out_specs=[
            pl.BlockSpec((PROJ_TM, n_a), row),
            pl.BlockSpec((PROJ_TM, D_NA), row),
            pl.BlockSpec((PROJ_TM, D_NA), row),
            pl.BlockSpec((PROJ_TM, D_NA), row),
        ],
        out_shape=[
            jax.ShapeDtypeStruct((t, n_a), F32),
            jax.ShapeDtypeStruct((t, D_NA), F32),
            jax.ShapeDtypeStruct((t, D_NA), BF16),
            jax.ShapeDtypeStruct((t, D_NA), BF16),
        ],
        compiler_params=pltpu.CompilerParams(
            dimension_semantics=("parallel",),
            vmem_limit_bytes=VMEM_LIMIT),
        name="in_proj",
    )(x, w_in)


def _pool_mixer(eu_ref, i, seq_len, pw_ref, ps_ref):
    half_lanes = LANES // 2
    lane = lax.broadcasted_iota(jnp.int32, (MIX_TM, LANES), 1)
    upper = lane >= half_lanes
    t = i * MIX_TM + lax.broadcasted_iota(jnp.int32, (MIX_TM, LANES), 0)

    def shifted(j, c0):
        return eu_ref[pl.ds(HALO + j, MIX_TM), c0:c0 + LANES]

    def window_sum(c0, w_small):
        hs = w_small // 2
        inner = shifted(-hs, c0)
        for j in range(-hs + 1, hs):
            inner = inner + shifted(j, c0)
        outer = shifted(-2 * hs, c0)
        for j in list(range(-2 * hs + 1, -hs)) + list(range(hs, 2 * hs)):
            outer = outer + shifted(j, c0)
        total = inner + jnp.where(upper, outer, 0.0)
        half = jnp.where(upper, 2 * hs, hs)
        cnt = jnp.minimum(t + half, seq_len) - jnp.maximum(t - half, 0)
        return total / cnt.astype(F32) - shifted(0, c0)

    p = jnp.concatenate([window_sum(0, POOL_WINDOWS[0]), window_sum(LANES, POOL_WINDOWS[2])], axis=-1)
    y = jnp.dot(p.astype(BF16), pw_ref[...], preferred_element_type=F32)
    return y * ps_ref[...]


def _mixer_kernel(seq_len,
                  pa_ref, hp_ref, hn_ref, q_ref, kp_ref, kc_ref, kn_ref, vp_ref, vc_ref, vn_ref,
                  bias_ref, pw_ref, ps_ref, cw_ref, x_ref, wo_ref, g_ref, b_ref,
                  o_ref, eu_ref, ez_ref, y_ref):
    i = pl.program_id(1)
    n_i = pl.num_programs(1)
    c_gb, c_gc, c_h = D_POOL, D_POOL + D_CONV, D_POOL + 2 * D_CONV

    hp = jnp.where(i > 0, hp_ref[...], 0.0)
    hn = jnp.where(i < n_i - 1, hn_ref[...], 0.0)
    eu_ref[0:HALO, :] = hp[:, 0:D_POOL]
    eu_ref[HALO:HALO + MIX_TM, :] = pa_ref[:, 0:D_POOL]
    eu_ref[HALO + MIX_TM:, :] = hn[:, 0:D_POOL]
    ez_ref[0:HALO, :] = hp[:, c_gc:c_h] * hp[:, c_h:]
    ez_ref[HALO:HALO + MIX_TM, :] = pa_ref[:, c_gc:c_h] * pa_ref[:, c_h:]
    ez_ref[HALO + MIX_TM:, :] = hn[:, c_gc:c_h] * hn[:, c_h:]

    y_a = _pool_mixer(eu_ref, i, seq_len, pw_ref, ps_ref)
    y_ref[:, :, 0:D_POOL] = y_a.reshape(ROWS_PER_BLOCK, GRID_W, D_POOL)

    conv = (cw_ref[0:1, :] * ez_ref[pl.ds(HALO - 1, MIX_TM), :]
            + cw_ref[1:2, :] * ez_ref[pl.ds(HALO, MIX_TM), :]
            + cw_ref[2:3, :] * ez_ref[pl.ds(HALO + 1, MIX_TM), :])
    y_b = pa_ref[:, c_gb:c_gc] * conv
    y_ref[:, :, D_POOL:D_POOL + D_CONV] = y_b.reshape(ROWS_PER_BLOCK, GRID_W, D_CONV)

    n_rows = seq_len // GRID_W
    half_rows = NA_ROWS // 2
    slab = HEADS_PER_GROUP * NA_HEAD_DIM
    head_of_lane = lax.broadcasted_iota(jnp.int32, (1, slab), 1) // NA_HEAD_DIM
    bias_row = 0
    for (qc0, mc, kc0) in COL_BLOCKS:
        m = ROWS_PER_BLOCK * mc
        ql = lax.broadcasted_iota(jnp.int32, (ROWS_PER_BLOCK, mc, KEY_ROWS * KEY_COLS), 0)
        kl = lax.broadcasted_iota(jnp.int32, (ROWS_PER_BLOCK, mc, KEY_ROWS * KEY_COLS), 2) // KEY_COLS
        r0 = i * ROWS_PER_BLOCK
        lo = jnp.clip(r0 + ql - half_rows, 0, n_rows - NA_ROWS) - (r0 - half_rows)
        row_mask = jnp.where((kl >= lo) & (kl < lo + NA_ROWS), 0.0, NEG_INF).reshape(m, KEY_ROWS * KEY_COLS)
        for hg in range(NA_HEADS // HEADS_PER_GROUP):
            l0 = hg * slab
            kwin = jnp.concatenate([kp_ref[half_rows:, kc0:kc0 + KEY_COLS, l0:l0 + slab],
                                    kc_ref[:, kc0:kc0 + KEY_COLS, l0:l0 + slab],
                                    kn_ref[0:half_rows, kc0:kc0 + KEY_COLS, l0:l0 + slab]],
                                   axis=0).reshape(KEY_ROWS * KEY_COLS, slab)
            vwin = jnp.concatenate([vp_ref[half_rows:, kc0:kc0 + KEY_COLS, l0:l0 + slab],
                                    vc_ref[:, kc0:kc0 + KEY_COLS, l0:l0 + slab],
                                    vn_ref[0:half_rows, kc0:kc0 + KEY_COLS, l0:l0 + slab]],
                                   axis=0).reshape(KEY_ROWS * KEY_COLS, slab)
            qt = q_ref[:, qc0:qc0 + mc, l0:l0 + slab].reshape(m, slab)
            acc = jnp.zeros((m, slab), F32)
            for hh in range(HEADS_PER_GROUP):
                head = hg * HEADS_PER_GROUP + hh
                sel = head_of_lane == hh
                qm = jnp.where(sel, qt, 0.0).astype(BF16)
                s = lax.dot_general(qm, kwin, (((1,), (1,)), ((), ())), preferred_element_type=F32)
                s = s + bias_ref[head, bias_row:bias_row + m, :] + row_mask
                s_max = jnp.max(s, axis=-1, keepdims=True)
                p = jnp.exp(s - s_max)
                denom = jnp.sum(p, axis=-1, keepdims=True)
                o = jnp.dot(p.astype(BF16), vwin, preferred_element_type=F32)
                acc = jnp.where(sel, o / denom, acc)
            c_na = D_POOL + D_CONV + l0
            y_ref[:, qc0:qc0 + mc, c_na:c_na + slab] = acc.reshape(ROWS_PER_BLOCK, mc, slab)
        bias_row += m

    y = y_ref[...].reshape(MIX_TM, D_MIX).astype(BF16)
    mixed = jnp.dot(y, wo_ref[...], preferred_element_type=F32)
    o_ref[...] = _layer_norm(ALPHA * x_ref[...] + mixed, g_ref[...], b_ref[...])


def _natten_bias_table(rpb):
    n_l, n_h = rpb.shape[:2]
    half_rows = NA_ROWS // 2
    pad_c = GRID_W - NA_COLS
    rp = jnp.pad(rpb, ((0, 0), (0, 0), (half_rows, half_rows), (pad_c, pad_c)), mode="edge")
    tabs = []
    for (qc0, mc, kc0) in COL_BLOCKS:
        starts = [kc0 - qc + (NA_COLS - 1) + pad_c for qc in range(qc0, qc0 + mc)]
        cols = jnp.stack([rp[..., s:s + KEY_COLS] for s in starts], axis=3)
        rows = jnp.stack([cols[:, :, NA_ROWS - 1 - ql:NA_ROWS - 1 - ql + KEY_ROWS]
                          for ql in range(ROWS_PER_BLOCK)], axis=2)
        tab = rows.transpose(0, 1, 2, 4, 3, 5).reshape(n_l, n_h, ROWS_PER_BLOCK * mc, KEY_ROWS * KEY_COLS)
        qc = qc0 + np.arange(mc)
        kc = kc0 + np.arange(KEY_COLS)
        cs = np.clip(qc - NA_COLS // 2, 0, GRID_W - NA_COLS)
        valid = (kc[None, :] >= cs[:, None]) & (kc[None, :] < cs[:, None] + NA_COLS)
        shape = (ROWS_PER_BLOCK, mc, KEY_ROWS, KEY_COLS)
        col_mask = np.where(np.broadcast_to(valid[None, :, None, :], shape), 0.0, NEG_INF)
        col_mask = col_mask.reshape(ROWS_PER_BLOCK * mc, -1).astype(np.float32)
        tabs.append(tab + col_mask)
    return jnp.concatenate(tabs, axis=2)


def _mixer(pa, q, k, v, bias, pool_w, pool_scale, conv_w, x, w_out, ln_g, ln_b, layer, batch, seq_len):
    t = x.shape[0]
    n_i = seq_len // MIX_TM
    assert seq_len % MIX_TM == 0 and n_i >= 2
    rows_total = t // GRID_W
    q3 = q.reshape(rows_total, GRID_W, D_NA)
    k3 = k.reshape(rows_total, GRID_W, D_NA)
    v3 = v.reshape(rows_total, GRID_W, D_NA)
    halo_blocks = MIX_TM // HALO
    n_a = pa.shape[1]
    ln_row = layer * 3 + 1

    blk = lambda b, i: (b * n_i + i, 0)
    blk3 = lambda b, i: (b * n_i + i, 0, 0)
    prev3 = lambda b, i: (b * n_i + jnp.maximum(i - 1, 0), 0, 0)
    next3 = lambda b, i: (b * n_i + jnp.minimum(i + 1, n_i - 1), 0, 0)
    halo_prev = lambda b, i: (jnp.maximum((b * n_i + i) * halo_blocks - 1, 0), 0)
    halo_next = lambda b, i: (jnp.minimum((b * n_i + i + 1) * halo_blocks, t // HALO - 1), 0)
    kv_spec = lambda im: pl.BlockSpec((ROWS_PER_BLOCK, GRID_W, D_NA), im)
    const2 = lambda b, i: (0, 0)

    return pl.pallas_call(
        functools.partial(_mixer_kernel, seq_len),
        grid=(batch, n_i),
        in_specs=[
            pl.BlockSpec((MIX_TM, n_a), blk),
            pl.BlockSpec((HALO, n_a), halo_prev),
            pl.BlockSpec((HALO, n_a), halo_next),
            kv_spec(blk3),
            kv_spec(prev3), kv_spec(blk3), kv_spec(next3),
            kv_spec(prev3), kv_spec(blk3), kv_spec(next3),
            pl.BlockSpec((None, NA_HEADS, MIX_TM, KEY_ROWS * KEY_COLS), lambda b, i: (layer, 0, 0, 0)),
            pl.BlockSpec((None, D_POOL, D_POOL), lambda b, i: (layer, 0, 0)),
            pl.BlockSpec((None, 1, D_POOL), lambda b, i: (layer, 0, 0)),
            pl.BlockSpec((None, 3, D_CONV), lambda b, i: (layer, 0, 0)),
            pl.BlockSpec((MIX_TM, D_MODEL), blk),
            pl.BlockSpec((None, D_MIX, D_MODEL), lambda b, i: (layer, 0, 0)),
            pl.BlockSpec((None, 1, D_MODEL), lambda b, i: (ln_row, 0, 0)),
            pl.BlockSpec((None, 1, D_MODEL), lambda b, i: (ln_row, 0, 0)),
        ],
        out_specs=pl.BlockSpec((MIX_TM, D_MODEL), blk),
        out_shape=jax.ShapeDtypeStruct((t, D_MODEL), F32),
        scratch_shapes=[
            pltpu.VMEM((MIX_TM + 2 * HALO, D_POOL), F32),
            pltpu.VMEM((MIX_TM + 2 * HALO, D_CONV), F32),
            pltpu.VMEM((ROWS_PER_BLOCK, GRID_W, D_MIX), F32),
        ],
        compiler_params=pltpu.CompilerParams(
            dimension_semantics=("parallel", "arbitrary"),
            vmem_limit_bytes=VMEM_LIMIT),
        name="mixer",
    )(pa, pa, pa, q3, k3, k3, k3, v3, v3, v3, bias, pool_w, pool_scale, conv_w, x, w_out, ln_g, ln_b)


def kernel(x, ffn1_w_gate, ffn1_w_up, ffn1_w_down, ffn2_w_gate, ffn2_w_up, ffn2_w_down,
           w_in, pool_w, pool_scale, conv_w, rpb, w_out, ln_g, ln_b):
    batch, seq_len, d = x.shape
    depth = w_in.shape[0]
    assert d == D_MODEL and depth == DEPTH and seq_len % GRID_W == 0
    t = batch * seq_len

    w1g, w1u, w1d = ffn1_w_gate.astype(BF16), ffn1_w_up.astype(BF16), ffn1_w_down.astype(BF16)
    w2g, w2u, w2d = ffn2_w_gate.astype(BF16), ffn2_w_up.astype(BF16), ffn2_w_down.astype(BF16)
    w_in_b = w_in.astype(BF16)
    w_out_b = w_out.astype(BF16)
    n_g = len(POOL_WINDOWS)
    pool_bd = (jnp.eye(n_g, dtype=F32)[None, :, None, :, None] * pool_w[:, :, :, None, :]
               ).reshape(depth, D_POOL, D_POOL).astype(BF16)
    pool_sc = pool_scale.reshape(depth, 1, D_POOL)
    bias = _natten_bias_table(rpb)
    g3 = ln_g.reshape(depth * 3, 1, D_MODEL)
    b3 = ln_b.reshape(depth * 3, 1, D_MODEL)

    h = x.reshape(t, D_MODEL)
    for l in range(depth):
        h = _ffn(h, w1g, w1u, w1d, g3, b3, l, 0)
        pa, q, k, v = _in_proj(h, w_in_b, l)
        h = _mixer(pa, q, k, v, bias, pool_bd, pool_sc, conv_w, h, w_out_b, g3, b3, l, batch, seq_len)
        h = _ffn(h, w2g, w2u, w2d, g3, b3, l, 2)
    return h.reshape(batch, seq_len, D_MODEL)
```

```python
import functools

import numpy as np
import jax
import jax.numpy as jnp
from jax import lax
from jax.experimental import pallas as pl
from jax.experimental.pallas import tpu as pltpu

D_MODEL = 1024
D_FF = 2816
D_POOL = 256
POOL_WINDOWS = (2, 4, 8, 16)
POOL_GROUP = D_POOL // len(POOL_WINDOWS)
D_CONV = 256
NA_HEADS = 8
NA_HEAD_DIM = 64
D_NA = NA_HEADS * NA_HEAD_DIM
D_MIX = D_POOL + D_CONV + D_NA
GRID_W = 64
NA_ROWS = 8
NA_COLS = 16
D_IN = D_POOL + 3 * D_CONV + 3 * D_NA
DEPTH = 4
ALPHA = (2.0 * DEPTH) ** 0.25
LN_EPS = 1e-5
NEG_INF = -1e30

V7X_VMEM_BYTES = 64 * 1024 * 1024
LANES = 128
SUBLANES_F32 = 8
SUBLANES_BF16 = 16
MXU_DIM = 256

BF16 = jnp.bfloat16
F32 = jnp.float32

FFN_TM = 512
FFN_TF = MXU_DIM
PROJ_TM = 512
ROWS_PER_BLOCK = NA_ROWS
MIX_TM = ROWS_PER_BLOCK * GRID_W
KEY_ROWS = 2 * NA_ROWS
HALO = SUBLANES_F32
HEADS_PER_GROUP = MXU_DIM // NA_HEAD_DIM
KEY_COLS = 2 * NA_COLS
COL_BLOCKS = ((0, 24, 0), (24, 16, 16), (40, 24, 32))
VMEM_LIMIT = 56 * 1024 * 1024

assert D_FF % FFN_TF == 0
assert sum(w for _, w, _ in COL_BLOCKS) == GRID_W
assert max(POOL_WINDOWS) // 2 <= HALO


def _layer_norm(y, g, b):
    mu = jnp.mean(y, axis=-1, keepdims=True)
    yc = y - mu
    var = jnp.mean(yc * yc, axis=-1, keepdims=True)
    return yc * lax.rsqrt(var + LN_EPS) * g + b


def _ffn_kernel(x_ref, wg_ref, wu_ref, wd_ref, g_ref, b_ref, o_ref, hid_ref):
    xb = x_ref[...].astype(BF16)
    for c in range(0, D_FF, FFN_TF):
        gate = jnp.dot(xb, wg_ref[:, c:c + FFN_TF], preferred_element_type=F32)
        up = jnp.dot(xb, wu_ref[:, c:c + FFN_TF], preferred_element_type=F32)
        hid_ref[:, c:c + FFN_TF] = ((gate * jax.nn.sigmoid(gate)) * up).astype(BF16)
    down = jnp.dot(hid_ref[...], wd_ref[...], preferred_element_type=F32)
    y = ALPHA * x_ref[...] + 0.5 * down
    o_ref[...] = _layer_norm(y, g_ref[...], b_ref[...])


def _resident(block_shape, index_map):
    return pl.BlockSpec(block_shape, index_map, pipeline_mode=pl.Buffered(1))


def _ffn(x, wg, wu, wd, ln_g, ln_b, layer, ln_idx):
    t = x.shape[0]
    assert t % FFN_TM == 0
    ln_row = layer * 3 + ln_idx
    return pl.pallas_call(
        _ffn_kernel,
        grid=(t // FFN_TM,),
        in_specs=[
            pl.BlockSpec((FFN_TM, D_MODEL), lambda i: (i, 0)),
            _resident((None, D_MODEL, D_FF), lambda i: (layer, 0, 0)),
            _resident((None, D_MODEL, D_FF), lambda i: (layer, 0, 0)),
            _resident((None, D_FF, D_MODEL), lambda i: (layer, 0, 0)),
            _resident((None, 1, D_MODEL), lambda i: (ln_row, 0, 0)),
            _resident((None, 1, D_MODEL), lambda i: (ln_row, 0, 0)),
        ],
        out_specs=pl.BlockSpec((FFN_TM, D_MODEL), lambda i: (i, 0)),
        out_shape=jax.ShapeDtypeStruct((t, D_MODEL), F32),
        scratch_shapes=[pltpu.VMEM((FFN_TM, D_FF), BF16)],
        compiler_params=pltpu.CompilerParams(
            dimension_semantics=("parallel",),
            vmem_limit_bytes=VMEM_LIMIT),
        name="ffn",
    )(x, wg, wu, wd, ln_g, ln_b)


def _in_proj_kernel(x_ref, w_ref, pa_ref, q_ref, k_ref, v_ref):
    xb = x_ref[...].astype(BF16)
    n_a = D_POOL + 3 * D_CONV
    pa_ref[...] = jnp.dot(xb, w_ref[:, 0:n_a], preferred_element_type=F32)
    q_ref[...] = jnp.dot(xb, w_ref[:, n_a:n_a + D_NA],
                         preferred_element_type=F32) * (NA_HEAD_DIM ** -0.5)
    k_ref[...] = jnp.dot(xb, w_ref[:, n_a + D_NA:n_a + 2 * D_NA],
                         preferred_element_type=F32).astype(BF16)
    v_ref[...] = jnp.dot(xb, w_ref[:, n_a + 2 * D_NA:n_a + 3 * D_NA],
                         preferred_element_type=F32).astype(BF16)


def _in_proj(x, w_in, layer):
    t = x.shape[0]
    assert t % PROJ_TM == 0
    n_a = D_POOL + 3 * D_CONV
    row = lambda i: (i, 0)
    return pl.pallas_call(
        _in_proj_kernel,
        grid=(t // PROJ_TM,),
        in_specs=[
            pl.BlockSpec((PROJ_TM, D_MODEL), row),
            pl.BlockSpec((None, D_MODEL, D_IN), lambda i: (layer, 0, 0)),
        ],
        out_specs=[
            pl.BlockSpec((PROJ_TM, n_a), row),
            pl.BlockSpec((PROJ_TM, D_NA), row),
            pl.BlockSpec((PROJ_TM, D_NA), row),
            pl.BlockSpec((PROJ_TM, D_NA), row),
        ],
        out_shape=[
            jax.ShapeDtypeStruct((t, n_a), F32),
            jax.ShapeDtypeStruct((t, D_NA), F32),
            jax.ShapeDtypeStruct((t, D_NA), BF16),
            jax.ShapeDtypeStruct((t, D_NA), BF16),
        ],
        compiler_params=pltpu.CompilerParams(
            dimension_semantics=("parallel",),
            vmem_limit_bytes=VMEM_LIMIT),
        name="in_proj",
    )(x, w_in)


def _pool_mixer(eu_ref, i, seq_len, pw_ref, ps_ref):
    half_lanes = LANES // 2
    lane = lax.broadcasted_iota(jnp.int32, (MIX_TM, LANES), 1)
    upper = lane >= half_lanes
    t = i * MIX_TM + lax.broadcasted_iota(jnp.int32, (MIX_TM, LANES), 0)

    def shifted(j, c0):
        return eu_ref[pl.ds(HALO + j, MIX_TM), c0:c0 + LANES]

    def window_sum(c0, w_small):
        hs = w_small // 2
        inner = shifted(-hs, c0)
        for j in range(-hs + 1, hs):
            inner = inner + shifted(j, c0)
        outer = shifted(-2 * hs, c0)
        for j in list(range(-2 * hs + 1, -hs)) + list(range(hs, 2 * hs)):
            outer = outer + shifted(j, c0)
        total = inner + jnp.where(upper, outer, 0.0)
        half = jnp.where(upper, 2 * hs, hs)
        cnt = jnp.minimum(t + half, seq_len) - jnp.maximum(t - half, 0)
        return total / cnt.astype(F32) - shifted(0, c0)

    p = jnp.concatenate([window_sum(0, POOL_WINDOWS[0]), window_sum(LANES, POOL_WINDOWS[2])], axis=-1)
    y = jnp.dot(p.astype(BF16), pw_ref[...], preferred_element_type=F32)
    return y * ps_ref[...]


def _mixer_kernel(seq_len,
                  pa_ref, hp_ref, hn_ref, q_ref, kp_ref, kc_ref, kn_ref, vp_ref, vc_ref, vn_ref,
                  bias_ref, pw_ref, ps_ref, cw_ref, x_ref, wo_ref, g_ref, b_ref,
                  o_ref, eu_ref, ez_ref, y_ref):
    i = pl.program_id(1)
    n_i = pl.num_programs(1)
    c_gb, c_gc, c_h = D_POOL, D_POOL + D_CONV, D_POOL + 2 * D_CONV

    hp = jnp.where(i > 0, hp_ref[...], 0.0)
    hn = jnp.where(i < n_i - 1, hn_ref[...], 0.0)
    eu_ref[0:HALO, :] = hp[:, 0:D_POOL]
    eu_ref[HALO:HALO + MIX_TM, :] = pa_ref[:, 0:D_POOL]
    eu_ref[HALO + MIX_TM:, :] = hn[:, 0:D_POOL]
    ez_ref[0:HALO, :] = hp[:, c_gc:c_h] * hp[:, c_h:]
    ez_ref[HALO:HALO + MIX_TM, :] = pa_ref[:, c_gc:c_h] * pa_ref[:, c_h:]
    ez_ref[HALO + MIX_TM:, :] = hn[:, c_gc:c_h] * hn[:, c_h:]

    y_a = _pool_mixer(eu_ref, i, seq_len, pw_ref, ps_ref)
    y_ref[:, :, 0:D_POOL] = y_a.reshape(ROWS_PER_BLOCK, GRID_W, D_POOL)

    conv = (cw_ref[0:1, :] * ez_ref[pl.ds(HALO - 1, MIX_TM), :]
            + cw_ref[1:2, :] * ez_ref[pl.ds(HALO, MIX_TM), :]
            + cw_ref[2:3, :] * ez_ref[pl.ds(HALO + 1, MIX_TM), :])
    y_b = pa_ref[:, c_gb:c_gc] * conv
    y_ref[:, :, D_POOL:D_POOL + D_CONV] = y_b.reshape(ROWS_PER_BLOCK, GRID_W, D_CONV)

    n_rows = seq_len // GRID_W
    half_rows = NA_ROWS // 2
    slab = HEADS_PER_GROUP * NA_HEAD_DIM
    head_of_lane = lax.broadcasted_iota(jnp.int32, (1, slab), 1) // NA_HEAD_DIM
    bias_row = 0
    for (qc0, mc, kc0) in COL_BLOCKS:
        m = ROWS_PER_BLOCK * mc
        ql = lax.broadcasted_iota(jnp.int32, (ROWS_PER_BLOCK, mc, KEY_ROWS * KEY_COLS), 0)
        kl = lax.broadcasted_iota(jnp.int32, (ROWS_PER_BLOCK, mc, KEY_ROWS * KEY_COLS), 2) // KEY_COLS
        r0 = i * ROWS_PER_BLOCK
        lo = jnp.clip(r0 + ql - half_rows, 0, n_rows - NA_ROWS) - (r0 - half_rows)
        row_mask = jnp.where((kl >= lo) & (kl < lo + NA_ROWS), 0.0, NEG_INF).reshape(m, KEY_ROWS * KEY_COLS)
        for hg in range(NA_HEADS // HEADS_PER_GROUP):
            l0 = hg * slab
            kwin = jnp.concatenate([kp_ref[half_rows:, kc0:kc0 + KEY_COLS, l0:l0 + slab],
                                    kc_ref[:, kc0:kc0 + KEY_COLS, l0:l0 + slab],
                                    kn_ref[0:half_rows, kc0:kc0 + KEY_COLS, l0:l0 + slab]],
                                   axis=0).reshape(KEY_ROWS * KEY_COLS, slab)
            vwin = jnp.concatenate([vp_ref[half_rows:, kc0:kc0 + KEY_COLS, l0:l0 + slab],
                                    vc_ref[:, kc0:kc0 + KEY_COLS, l0:l0 + slab],
                                    vn_ref[0:half_rows, kc0:kc0 + KEY_COLS, l0:l0 + slab]],
                                   axis=0).reshape(KEY_ROWS * KEY_COLS, slab)
            qt = q_ref[:, qc0:qc0 + mc, l0:l0 + slab].reshape(m, slab)
            acc = jnp.zeros((m, slab), F32)
            for hh in range(HEADS_PER_GROUP):
                head = hg * HEADS_PER_GROUP + hh
                sel = head_of_lane == hh
                qm = jnp.where(sel, qt, 0.0).astype(BF16)
                s = lax.dot_general(qm, kwin, (((1,), (1,)), ((), ())), preferred_element_type=F32)
                s = s + bias_ref[head, bias_row:bias_row + m, :] + row_mask
                s_max = jnp.max(s, axis=-1, keepdims=True)
                p = jnp.exp(s - s_max)
                denom = jnp.sum(p, axis=-1, keepdims=True)
                o = jnp.dot(p.astype(BF16), vwin, preferred_element_type=F32)
                acc = jnp.where(sel, o / denom, acc)
            c_na = D_POOL + D_CONV + l0
            y_ref[:, qc0:qc0 + mc, c_na:c_na + slab] = acc.reshape(ROWS_PER_BLOCK, mc, slab)
        bias_row += m

    y = y_ref[...].reshape(MIX_TM, D_MIX).astype(BF16)
    mixed = jnp.dot(y, wo_ref[...], preferred_element_type=F32)
    o_ref[...] = _layer_norm(ALPHA * x_ref[...] + mixed, g_ref[...], b_ref[...])


def _natten_bias_table(rpb):
    n_l, n_h = rpb.shape[:2]
    half_rows = NA_ROWS // 2
    pad_c = GRID_W - NA_COLS
    rp = jnp.pad(rpb, ((0, 0), (0, 0), (half_rows, half_rows), (pad_c, pad_c)), mode="edge")
    tabs = []
    for (qc0, mc, kc0) in COL_BLOCKS:
        starts = [kc0 - qc + (NA_COLS - 1) + pad_c for qc in range(qc0, qc0 + mc)]
        cols = jnp.stack([rp[..., s:s + KEY_COLS] for s in starts], axis=3)
        rows = jnp.stack([cols[:, :, NA_ROWS - 1 - ql:NA_ROWS - 1 - ql + KEY_ROWS]
                          for ql in range(ROWS_PER_BLOCK)], axis=2)
        tab = rows.transpose(0, 1, 2, 4, 3, 5).reshape(n_l, n_h, ROWS_PER_BLOCK * mc, KEY_ROWS * KEY_COLS)
        qc = qc0 + np.arange(mc)
        kc = kc0 + np.arange(KEY_COLS)
        cs = np.clip(qc - NA_COLS // 2, 0, GRID_W - NA_COLS)
        valid = (kc[None, :] >= cs[:, None]) & (kc[None, :] < cs[:, None] + NA_COLS)
        shape = (ROWS_PER_BLOCK, mc, KEY_ROWS, KEY_COLS)
        col_mask = np.where(np.broadcast_to(valid[None, :, None, :], shape), 0.0, NEG_INF)
        col_mask = col_mask.reshape(ROWS_PER_BLOCK * mc, -1).astype(np.float32)
        tabs.append(tab + col_mask)
    return jnp.concatenate(tabs, axis=2)


def _mixer(pa, q, k, v, bias, pool_w, pool_scale, conv_w, x, w_out, ln_g, ln_b, layer, batch, seq_len):
    t = x.shape[0]
    n_i = seq_len // MIX_TM
    assert seq_len % MIX_TM == 0 and n_i >= 2
    rows_total = t // GRID_W
    q3 = q.reshape(rows_total, GRID_W, D_NA)
    k3 = k.reshape(rows_total, GRID_W, D_NA)
    v3 = v.reshape(rows_total, GRID_W, D_NA)
    halo_blocks = MIX_TM // HALO
    n_a = pa.shape[1]
    ln_row = layer * 3 + 1

    blk = lambda b, i: (b * n_i + i, 0)
    blk3 = lambda b, i: (b * n_i + i, 0, 0)
    prev3 = lambda b, i: (b * n_i + jnp.maximum(i - 1, 0), 0, 0)
    next3 = lambda b, i: (b * n_i + jnp.minimum(i + 1, n_i - 1), 0, 0)
    halo_prev = lambda b, i: (jnp.maximum((b * n_i + i) * halo_blocks - 1, 0), 0)
    halo_next = lambda b, i: (jnp.minimum((b * n_i + i + 1) * halo_blocks, t // HALO - 1), 0)
    kv_spec = lambda im: pl.BlockSpec((ROWS_PER_BLOCK, GRID_W, D_NA), im)
    const2 = lambda b, i: (0, 0)

    return pl.pallas_call(
        functools.partial(_mixer_kernel, seq_len),
        grid=(batch, n_i),
        in_specs=[
            pl.BlockSpec((MIX_TM, n_a), blk),
            pl.BlockSpec((HALO, n_a), halo_prev),
            pl.BlockSpec((HALO, n_a), halo_next),
            kv_spec(blk3),
            kv_spec(prev3), kv_spec(blk3), kv_spec(next3),
            kv_spec(prev3), kv_spec(blk3), kv_spec(next3),
            pl.BlockSpec((None, NA_HEADS, MIX_TM, KEY_ROWS * KEY_COLS), lambda b, i: (layer, 0, 0, 0)),
            pl.BlockSpec((None, D_POOL, D_POOL), lambda b, i: (layer, 0, 0)),
            pl.BlockSpec((None, 1, D_POOL), lambda b, i: (layer, 0, 0)),
            pl.BlockSpec((None, 3, D_CONV), lambda b, i: (layer, 0, 0)),
            pl.BlockSpec((MIX_TM, D_MODEL), blk),
            pl.BlockSpec((None, D_MIX, D_MODEL), lambda b, i: (layer, 0, 0)),
            pl.BlockSpec((None, 1, D_MODEL), lambda b, i: (ln_row, 0, 0)),
            pl.BlockSpec((None, 1, D_MODEL), lambda b, i: (ln_row, 0, 0)),
        ],
        out_specs=pl.BlockSpec((MIX_TM, D_MODEL), blk),
        out_shape=jax.ShapeDtypeStruct((t, D_MODEL), F32),
        scratch_shapes=[
            pltpu.VMEM((MIX_TM + 2 * HALO, D_POOL), F32),
            pltpu.VMEM((MIX_TM + 2 * HALO, D_CONV), F32),
            pltpu.VMEM((ROWS_PER_BLOCK, GRID_W, D_MIX), F32),
        ],
        compiler_params=pltpu.CompilerParams(
            dimension_semantics=("parallel", "arbitrary"),
            vmem_limit_bytes=VMEM_LIMIT),
        name="mixer",
    )(pa, pa, pa, q3, k3, k3, k3, v3, v3, v3, bias, pool_w, pool_scale, conv_w, x, w_out, ln_g, ln_b)


def kernel(x, ffn1_w_gate, ffn1_w_up, ffn1_w_down, ffn2_w_gate, ffn2_w_up, ffn2_w_down,
           w_in, pool_w, pool_scale, conv_w, rpb, w_out, ln_g, ln_b):
    batch, seq_len, d = x.shape
    depth = w_in.shape[0]
    assert d == D_MODEL and depth == DEPTH and seq_len % GRID_W == 0
    t = batch * seq_len

    w1g, w1u, w1d = ffn1_w_gate.astype(BF16), ffn1_w_up.astype(BF16), ffn1_w_down.astype(BF16)
    w2g, w2u, w2d = ffn2_w_gate.astype(BF16), ffn2_w_up.astype(BF16), ffn2_w_down.astype(BF16)
    w_in_b = w_in.astype(BF16)
    w_out_b = w_out.astype(BF16)
    n_g = len(POOL_WINDOWS)
    pool_bd = (jnp.eye(n_g, dtype=F32)[None, :, None, :, None] * pool_w[:, :, :, None, :]
               ).reshape(depth, D_POOL, D_POOL).astype(BF16)
    pool_sc = pool_scale.reshape(depth, 1, D_POOL)
    bias = _natten_bias_table(rpb)
    g3 = ln_g.reshape(depth * 3, 1, D_MODEL)
    b3 = ln_b.reshape(depth * 3, 1, D_MODEL)

    h = x.reshape(t, D_MODEL)
    for l in range(depth):
        h = _ffn(h, w1g, w1u, w1d, g3, b3, l, 0)
        pa, q, k, v = _in_proj(h, w_in_b, l)
        h = _mixer(pa, q, k, v, bias, pool_bd, pool_sc, conv_w, h, w_out_b, g3, b3, l, batch, seq_len)
        h = _ffn(h, w2g, w2u, w2d, g3, b3, l, 2)
    return h.reshape(batch, seq_len, D_MODEL)
```

```python
import functools

import numpy as np
import jax
import jax.numpy as jnp
from jax import lax
from jax.experimental import pallas as pl
from jax.experimental.pallas import tpu as pltpu

D_MODEL = 1024
D_FF = 2816
D_POOL = 256
POOL_WINDOWS = (2, 4, 8, 16)
POOL_GROUP = D_POOL // len(POOL_WINDOWS)
D_CONV = 256
NA_HEADS = 8
NA_HEAD_DIM = 64
D_NA = NA_HEADS * NA_HEAD_DIM
D_MIX = D_POOL + D_CONV + D_NA
GRID_W = 64
NA_ROWS = 8
NA_COLS = 16
D_IN = D_POOL + 3 * D_CONV + 3 * D_NA
DEPTH = 4
ALPHA = (2.0 * DEPTH) ** 0.25
LN_EPS = 1e-5
NEG_INF = -1e30
LOG2E = 1.4426950408889634

V7X_VMEM_BYTES = 64 * 1024 * 1024
LANES = 128
SUBLANES_F32 = 8
SUBLANES_BF16 = 16
MXU_DIM = 256

BF16 = jnp.bfloat16
F32 = jnp.float32

FFN_TM = 512
FFN_TF = MXU_DIM
PROJ_TM = 512
ROWS_PER_BLOCK = NA_ROWS
MIX_TM = ROWS_PER_BLOCK * GRID_W
KEY_ROWS = 2 * NA_ROWS
HALO = SUBLANES_F32
HEADS_PER_GROUP = MXU_DIM // NA_HEAD_DIM
KEY_COLS = 2 * NA_COLS
COL_BLOCKS = ((0, 24, 0), (24, 16, 16), (40, 24, 32))
VMEM_LIMIT = 56 * 1024 * 1024

assert D_FF % FFN_TF == 0
assert sum(w for _, w, _ in COL_BLOCKS) == GRID_W
assert max(POOL_WINDOWS) // 2 <= HALO


def _layer_norm(y, g, b):
    mu = jnp.mean(y, axis=-1, keepdims=True)
    yc = y - mu
    var = jnp.mean(yc * yc, axis=-1, keepdims=True)
    return yc * lax.rsqrt(var + LN_EPS) * g + b


def _ffn_kernel(x_ref, wg_ref, wu_ref, wd_ref, g_ref, b_ref, o_ref, hid_ref):
    xb = x_ref[...].astype(BF16)
    for c in range(0, D_FF, FFN_TF):
        gate = jnp.dot(xb, wg_ref[:, c:c + FFN_TF], preferred_element_type=F32)
        up = jnp.dot(xb, wu_ref[:, c:c + FFN_TF], preferred_element_type=F32)
        hid_ref[:, c:c + FFN_TF] = ((gate * jax.nn.sigmoid(gate)) * up).astype(BF16)
    down = jnp.dot(hid_ref[...], wd_ref[...], preferred_element_type=F32)
    y = ALPHA * x_ref[...] + 0.5 * down
    o_ref[...] = _layer_norm(y, g_ref[...], b_ref[...])


def _resident(block_shape, index_map):
    return pl.BlockSpec(block_shape, index_map, pipeline_mode=pl.Buffered(1))


def _ffn(x, wg, wu, wd, ln_g, ln_b, layer, ln_idx):
    t = x.shape[0]
    assert t % FFN_TM == 0
    ln_row = layer * 3 + ln_idx
    return pl.pallas_call(
        _ffn_kernel,
        grid=(t // FFN_TM,),
        in_specs=[
            pl.BlockSpec((FFN_TM, D_MODEL), lambda i: (i, 0)),
            _resident((None, D_MODEL, D_FF), lambda i: (layer, 0, 0)),
            _resident((None, D_MODEL, D_FF), lambda i: (layer, 0, 0)),
            _resident((None, D_FF, D_MODEL), lambda i: (layer, 0, 0)),
            _resident((None, 1, D_MODEL), lambda i: (ln_row, 0, 0)),
            _resident((None, 1, D_MODEL), lambda i: (ln_row, 0, 0)),
        ],
        out_specs=pl.BlockSpec((FFN_TM, D_MODEL), lambda i: (i, 0)),
        out_shape=jax.ShapeDtypeStruct((t, D_MODEL), F32),
        scratch_shapes=[pltpu.VMEM((FFN_TM, D_FF), BF16)],
        compiler_params=pltpu.CompilerParams(
            dimension_semantics=("parallel",),
            vmem_limit_bytes=VMEM_LIMIT),
        name="ffn",
    )(x, wg, wu, wd, ln_g, ln_b)


def _in_proj_kernel(x_ref, w_ref, pa_ref, q_ref, k_ref, v_ref):
    xb = x_ref[...].astype(BF16)
    n_a = D_POOL + 3 * D_CONV
    pa_ref[...] = jnp.dot(xb, w_ref[:, 0:n_a], preferred_element_type=F32)
    q_ref[...] = jnp.dot(xb, w_ref[:, n_a:n_a + D_NA],
                         preferred_element_type=F32) * (NA_HEAD_DIM ** -0.5 * LOG2E)
    k_ref[...] = jnp.dot(xb, w_ref[:, n_a + D_NA:n_a + 2 * D_NA],
                         preferred_element_type=F32).astype(BF16)
    v_ref[...] = jnp.dot(xb, w_ref[:, n_a + 2 * D_NA:n_a + 3 * D_NA],
                         preferred_element_type=F32).astype(BF16)


def _in_proj(x, w_in, layer):
    t = x.shape[0]
    assert t % PROJ_TM == 0
    n_a = D_POOL + 3 * D_CONV
    row = lambda i: (i, 0)
    return pl.pallas_call(
        _in_proj_kernel,
        grid=(t // PROJ_TM,),
        in_specs=[
            pl.BlockSpec((PROJ_TM, D_MODEL), row),
            pl.BlockSpec((None, D_MODEL, D_IN), lambda i: (layer, 0, 0)),
        ],
        out_specs=[
            pl.BlockSpec((PROJ_TM, n_a), row),
            pl.BlockSpec((PROJ_TM, D_NA), row),
            pl.BlockSpec((PROJ_TM, D_NA), row),
            pl.BlockSpec((PROJ_TM, D_NA), row),
        ],
        out_shape=[
            jax.ShapeDtypeStruct((t, n_a), F32),
            jax.ShapeDtypeStruct((t, D_NA), F32),
            jax.ShapeDtypeStruct((t, D_NA), BF16),
            jax.ShapeDtypeStruct((t, D_NA), BF16),
        ],
        compiler_params=pltpu.CompilerParams(
            dimension_semantics=("parallel",),
            vmem_limit_bytes=VMEM_LIMIT),
        name="in_proj",
    )(x, w_in)


def _pool_mixer(eu_ref, i, seq_len, pw_ref, ps_ref):
    half_lanes = LANES // 2
    lane = lax.broadcasted_iota(jnp.int32, (MIX_TM, LANES), 1)
    upper = lane >= half_lanes
    t = i * MIX_TM + lax.broadcasted_iota(jnp.int32, (MIX_TM, LANES), 0)

    def shifted(j, c0):
        return eu_ref[pl.ds(HALO + j, MIX_TM), c0:c0 + LANES]

    def window_sum(c0, w_small):
        hs = w_small // 2
        inner = shifted(-hs, c0)
        for j in range(-hs + 1, hs):
            inner = inner + shifted(j, c0)
        outer = shifted(-2 * hs, c0)
        for j in list(range(-2 * hs + 1, -hs)) + list(range(hs, 2 * hs)):
            outer = outer + shifted(j, c0)
        total = inner + jnp.where(upper, outer, 0.0)
        half = jnp.where(upper, 2 * hs, hs)
        cnt = jnp.minimum(t + half, seq_len) - jnp.maximum(t - half, 0)
        return total / cnt.astype(F32) - shifted(0, c0)

    p = jnp.concatenate([window_sum(0, POOL_WINDOWS[0]), window_sum(LANES, POOL_WINDOWS[2])], axis=-1)
    y = jnp.dot(p.astype(BF16), pw_ref[...], preferred_element_type=F32)
    return y * ps_ref[...]


def _mixer_kernel(seq_len,
                  pa_ref, hp_ref, hn_ref, q_ref, kp_ref, kc_ref, kn_ref, vp_ref, vc_ref, vn_ref,
                  bias_ref, pw_ref, ps_ref, cw_ref, x_ref, wo_ref, g_ref, b_ref,
                  o_ref, eu_ref, ez_ref, y_ref):
    i = pl.program_id(1)
    n_i = pl.num_programs(1)
    c_gb, c_gc, c_h = D_POOL, D_POOL + D_CONV, D_POOL + 2 * D_CONV

    hp = jnp.where(i > 0, hp_ref[...], 0.0)
    hn = jnp.where(i < n_i - 1, hn_ref[...], 0.0)
    eu_ref[0:HALO, :] = hp[:, 0:D_POOL]
    eu_ref[HALO:HALO + MIX_TM, :] = pa_ref[:, 0:D_POOL]
    eu_ref[HALO + MIX_TM:, :] = hn[:, 0:D_POOL]
    ez_ref[0:HALO, :] = hp[:, c_gc:c_h] * hp[:, c_h:]
    ez_ref[HALO:HALO + MIX_TM, :] = pa_ref[:, c_gc:c_h] * pa_ref[:, c_h:]
    ez_ref[HALO + MIX_TM:, :] = hn[:, c_gc:c_h] * hn[:, c_h:]

    y_a = _pool_mixer(eu_ref, i, seq_len, pw_ref, ps_ref)
    y_ref[:, :, 0:D_POOL] = y_a.reshape(ROWS_PER_BLOCK, GRID_W, D_POOL)

    conv = (cw_ref[0:1, :] * ez_ref[pl.ds(HALO - 1, MIX_TM), :]
            + cw_ref[1:2, :] * ez_ref[pl.ds(HALO, MIX_TM), :]
            + cw_ref[2:3, :] * ez_ref[pl.ds(HALO + 1, MIX_TM), :])
    y_b = pa_ref[:, c_gb:c_gc] * conv
    y_ref[:, :, D_POOL:D_POOL + D_CONV] = y_b.reshape(ROWS_PER_BLOCK, GRID_W, D_CONV)

    half_rows = NA_ROWS // 2
    slab = HEADS_PER_GROUP * NA_HEAD_DIM
    head_of_lane = lax.broadcasted_iota(jnp.int32, (1, slab), 1) // NA_HEAD_DIM
    bias_row = 0
    for (qc0, mc, kc0) in COL_BLOCKS:
        m = ROWS_PER_BLOCK * mc
        for hg in range(NA_HEADS // HEADS_PER_GROUP):
            l0 = hg * slab
            kwin = jnp.concatenate([kp_ref[half_rows:, kc0:kc0 + KEY_COLS, l0:l0 + slab],
                                    kc_ref[:, kc0:kc0 + KEY_COLS, l0:l0 + slab],
                                    kn_ref[0:half_rows, kc0:kc0 + KEY_COLS, l0:l0 + slab]],
                                   axis=0).reshape(KEY_ROWS * KEY_COLS, slab)
            vwin = jnp.concatenate([vp_ref[half_rows:, kc0:kc0 + KEY_COLS, l0:l0 + slab],
                                    vc_ref[:, kc0:kc0 + KEY_COLS, l0:l0 + slab],
                                    vn_ref[0:half_rows, kc0:kc0 + KEY_COLS, l0:l0 + slab]],
                                   axis=0).reshape(KEY_ROWS * KEY_COLS, slab)
            qt = q_ref[:, qc0:qc0 + mc, l0:l0 + slab].reshape(m, slab)
            sels = [head_of_lane == hh for hh in range(HEADS_PER_GROUP)]
            q4 = jnp.concatenate([jnp.where(sel, qt, 0.0) for sel in sels], axis=0).astype(BF16)
            s = lax.dot_general(q4, kwin, (((1,), (1,)), ((), ())), preferred_element_type=F32)
            s = s + bias_ref[bias_row:bias_row + HEADS_PER_GROUP * m, :]
            p = jnp.exp2(s - jnp.max(s, axis=-1, keepdims=True))
            inv = 1.0 / jnp.sum(p, axis=-1, keepdims=True)
            o4 = jnp.dot(p.astype(BF16), vwin, preferred_element_type=F32) * inv
            acc = o4[(HEADS_PER_GROUP - 1) * m:]
            for hh in range(HEADS_PER_GROUP - 2, -1, -1):
                acc = jnp.where(sels[hh], o4[hh * m:(hh + 1) * m], acc)
            c_na = D_POOL + D_CONV + l0
            y_ref[:, qc0:qc0 + mc, c_na:c_na + slab] = acc.reshape(ROWS_PER_BLOCK, mc, slab)
            bias_row += HEADS_PER_GROUP * m

    y = y_ref[...].reshape(MIX_TM, D_MIX).astype(BF16)
    mixed = jnp.dot(y, wo_ref[...], preferred_element_type=F32)
    o_ref[...] = _layer_norm(ALPHA * x_ref[...] + mixed, g_ref[...], b_ref[...])


def _row_lo(ql, variant):
    half_rows = NA_ROWS // 2
    if variant == 0:
        return np.maximum(ql - half_rows, 0) + half_rows
    if variant == 2:
        return np.minimum(ql, half_rows)
    return ql


def _natten_bias_table(rpb):
    n_l, n_h = rpb.shape[:2]
    half_rows = NA_ROWS // 2
    pad_c = GRID_W - NA_COLS
    rp = jnp.pad(rpb, ((0, 0), (0, 0), (half_rows, half_rows), (pad_c, pad_c)), mode="edge")
    tabs = []
    for (qc0, mc, kc0) in COL_BLOCKS:
        starts = [kc0 - qc + (NA_COLS - 1) + pad_c for qc in range(qc0, qc0 + mc)]
        cols = jnp.stack([rp[..., s:s + KEY_COLS] for s in starts], axis=3)
        rows = jnp.stack([cols[:, :, NA_ROWS - 1 - ql:NA_ROWS - 1 - ql + KEY_ROWS]
                          for ql in range(ROWS_PER_BLOCK)], axis=2)
        tab = rows.transpose(0, 1, 2, 4, 3, 5).reshape(n_l, n_h, ROWS_PER_BLOCK * mc, KEY_ROWS * KEY_COLS)
        qc = qc0 + np.arange(mc)
        kc = kc0 + np.arange(KEY_COLS)
        cs = np.clip(qc - NA_COLS // 2, 0, GRID_W - NA_COLS)
        valid = (kc[None, :] >= cs[:, None]) & (kc[None, :] < cs[:, None] + NA_COLS)
        shape = (ROWS_PER_BLOCK, mc, KEY_ROWS, KEY_COLS)
        ql = np.arange(ROWS_PER_BLOCK)[:, None]
        kl = np.arange(KEY_ROWS)[None, :]
        masks = []
        for variant in range(3):
            lo = _row_lo(ql, variant)
            row_valid = (kl >= lo) & (kl < lo + NA_ROWS)
            ok = np.broadcast_to(valid[None, :, None, :] & row_valid[:, None, :, None], shape)
            masks.append(np.where(ok, 0.0, NEG_INF).reshape(ROWS_PER_BLOCK * mc, -1).astype(np.float32))
        mask = np.stack(masks)[None, :, None]
        tab = tab[:, None] * LOG2E + mask
        tabs.append(tab.reshape(n_l, 3, n_h * ROWS_PER_BLOCK * mc, KEY_ROWS * KEY_COLS))
    return jnp.concatenate(tabs, axis=2)


def _mixer(pa, q, k, v, bias, pool_w, pool_scale, conv_w, x, w_out, ln_g, ln_b, layer, batch, seq_len):
    t = x.shape[0]
    n_i = seq_len // MIX_TM
    assert seq_len % MIX_TM == 0 and n_i >= 2
    rows_total = t // GRID_W
    q3 = q.reshape(rows_total, GRID_W, D_NA)
    k3 = k.reshape(rows_total, GRID_W, D_NA)
    v3 = v.reshape(rows_total, GRID_W, D_NA)
    halo_blocks = MIX_TM // HALO
    n_a = pa.shape[1]
    ln_row = layer * 3 + 1

    blk = lambda b, i: (b * n_i + i, 0)
    blk3 = lambda b, i: (b * n_i + i, 0, 0)
    prev3 = lambda b, i: (b * n_i + jnp.maximum(i - 1, 0), 0, 0)
    next3 = lambda b, i: (b * n_i + jnp.minimum(i + 1, n_i - 1), 0, 0)
    halo_prev = lambda b, i: (jnp.maximum((b * n_i + i) * halo_blocks - 1, 0), 0)
    halo_next = lambda b, i: (jnp.minimum((b * n_i + i + 1) * halo_blocks, t // HALO - 1), 0)
    kv_spec = lambda im: pl.BlockSpec((ROWS_PER_BLOCK, GRID_W, D_NA), im)
    const2 = lambda b, i: (0, 0)

    return pl.pallas_call(
        functools.partial(_mixer_kernel, seq_len),
        grid=(batch, n_i),
        in_specs=[
            pl.BlockSpec((MIX_TM, n_a), blk),
            pl.BlockSpec((HALO, n_a), halo_prev),
            pl.BlockSpec((HALO, n_a), halo_next),
            kv_spec(blk3),
            kv_spec(prev3), kv_spec(blk3), kv_spec(next3),
            kv_spec(prev3), kv_spec(blk3), kv_spec(next3),
            pl.BlockSpec((None, None, NA_HEADS * MIX_TM, KEY_ROWS * KEY_COLS),
                         lambda b, i: (layer, (i > 0).astype(jnp.int32) + (i == n_i - 1).astype(jnp.int32), 0, 0)),
            pl.BlockSpec((None, D_POOL, D_POOL), lambda b, i: (layer, 0, 0)),
            pl.BlockSpec((None, 1, D_POOL), lambda b, i: (layer, 0, 0)),
            pl.BlockSpec((None, 3, D_CONV), lambda b, i: (layer, 0, 0)),
            pl.BlockSpec((MIX_TM, D_MODEL), blk),
            pl.BlockSpec((None, D_MIX, D_MODEL), lambda b, i: (layer, 0, 0)),
            pl.BlockSpec((None, 1, D_MODEL), lambda b, i: (ln_row, 0, 0)),
            pl.BlockSpec((None, 1, D_MODEL), lambda b, i: (ln_row, 0, 0)),
        ],
        out_specs=pl.BlockSpec((MIX_TM, D_MODEL), blk),
        out_shape=jax.ShapeDtypeStruct((t, D_MODEL), F32),
        scratch_shapes=[
            pltpu.VMEM((MIX_TM + 2 * HALO, D_POOL), F32),
            pltpu.VMEM((MIX_TM + 2 * HALO, D_CONV), F32),
            pltpu.VMEM((ROWS_PER_BLOCK, GRID_W, D_MIX), F32),
        ],
        compiler_params=pltpu.CompilerParams(
            dimension_semantics=("parallel", "arbitrary"),
            vmem_limit_bytes=VMEM_LIMIT),
        name="mixer",
    )(pa, pa, pa, q3, k3, k3, k3, v3, v3, v3, bias, pool_w, pool_scale, conv_w, x, w_out, ln_g, ln_b)


def kernel(x, ffn1_w_gate, ffn1_w_up, ffn1_w_down, ffn2_w_gate, ffn2_w_up, ffn2_w_down,
           w_in, pool_w, pool_scale, conv_w, rpb, w_out, ln_g, ln_b):
    batch, seq_len, d = x.shape
    depth = w_in.shape[0]
    assert d == D_MODEL and depth == DEPTH and seq_len % GRID_W == 0
    t = batch * seq_len

    w1g, w1u, w1d = ffn1_w_gate.astype(BF16), ffn1_w_up.astype(BF16), ffn1_w_down.astype(BF16)
    w2g, w2u, w2d = ffn2_w_gate.astype(BF16), ffn2_w_up.astype(BF16), ffn2_w_down.astype(BF16)
    w_in_b = w_in.astype(BF16)
    w_out_b = w_out.astype(BF16)
    n_g = len(POOL_WINDOWS)
    pool_bd = (jnp.eye(n_g, dtype=F32)[None, :, None, :, None] * pool_w[:, :, :, None, :]
               ).reshape(depth, D_POOL, D_POOL).astype(BF16)
    pool_sc = pool_scale.reshape(depth, 1, D_POOL)
    bias = _natten_bias_table(rpb)
    g3 = ln_g.reshape(depth * 3, 1, D_MODEL)
    b3 = ln_b.reshape(depth * 3, 1, D_MODEL)

    h = x.reshape(t, D_MODEL)
    for l in range(depth):
        h = _ffn(h, w1g, w1u, w1d, g3, b3, l, 0)
        pa, q, k, v = _in_proj(h, w_in_b, l)
        h = _mixer(pa, q, k, v, bias, pool_bd, pool_sc, conv_w, h, w_out_b, g3, b3, l, batch, seq_len)
        h = _ffn(h, w2g, w2u, w2d, g3, b3, l, 2)
    return h.reshape(batch, seq_len, D_MODEL)
```

```python
import functools

import numpy as np
import jax
import jax.numpy as jnp
from jax import lax
from jax.experimental import pallas as pl
from jax.experimental.pallas import tpu as pltpu

D_MODEL = 1024
D_FF = 2816
D_POOL = 256
POOL_WINDOWS = (2, 4, 8, 16)
POOL_GROUP = D_POOL // len(POOL_WINDOWS)
D_CONV = 256
NA_HEADS = 8
NA_HEAD_DIM = 64
D_NA = NA_HEADS * NA_HEAD_DIM
D_MIX = D_POOL + D_CONV + D_NA
GRID_W = 64
NA_ROWS = 8
NA_COLS = 16
D_IN = D_POOL + 3 * D_CONV + 3 * D_NA
DEPTH = 4
ALPHA = (2.0 * DEPTH) ** 0.25
LN_EPS = 1e-5
NEG_INF = -1e30
LOG2E = 1.4426950408889634

V7X_VMEM_BYTES = 64 * 1024 * 1024
LANES = 128
SUBLANES_F32 = 8
SUBLANES_BF16 = 16
MXU_DIM = 256

BF16 = jnp.bfloat16
F32 = jnp.float32

FFN_TM = 512
FFN_TF = MXU_DIM
PROJ_TM = 512
ROWS_PER_BLOCK = NA_ROWS
MIX_TM = ROWS_PER_BLOCK * GRID_W
KEY_ROWS = 2 * NA_ROWS
HALO = SUBLANES_F32
HEADS_PER_GROUP = MXU_DIM // NA_HEAD_DIM
KEY_COLS = 2 * NA_COLS
N_KEYS = KEY_ROWS * KEY_COLS
COL_BLOCKS = ((0, 24, 0), (24, 16, 16), (40, 24, 32))
VMEM_LIMIT = 56 * 1024 * 1024

assert D_FF % FFN_TF == 0
assert sum(w for _, w, _ in COL_BLOCKS) == GRID_W
assert max(POOL_WINDOWS) // 2 <= HALO


def _layer_norm(y, g, b):
    mu = jnp.mean(y, axis=-1, keepdims=True)
    yc = y - mu
    var = jnp.mean(yc * yc, axis=-1, keepdims=True)
    return yc * lax.rsqrt(var + LN_EPS) * g + b


def _resident(block_shape, index_map):
    return pl.BlockSpec(block_shape, index_map, pipeline_mode=pl.Buffered(1))


def _ffn_kernel(x_ref, wg_ref, wu_ref, wd_ref, g_ref, b_ref, o_ref, hid_ref):
    xb = x_ref[...].astype(BF16)
    for c in range(0, D_FF, FFN_TF):
        gate = jnp.dot(xb, wg_ref[:, c:c + FFN_TF], preferred_element_type=F32)
        up = jnp.dot(xb, wu_ref[:, c:c + FFN_TF], preferred_element_type=F32)
        hid_ref[:, c:c + FFN_TF] = ((gate * jax.nn.sigmoid(gate)) * up).astype(BF16)
    down = jnp.dot(hid_ref[...], wd_ref[...], preferred_element_type=F32)
    y = ALPHA * x_ref[...] + 0.5 * down
    o_ref[...] = _layer_norm(y, g_ref[...], b_ref[...])


def _ffn(x, wg, wu, wd, ln_g, ln_b, layer, ln_idx):
    t = x.shape[0]
    assert t % FFN_TM == 0
    ln_row = layer * 3 + ln_idx
    return pl.pallas_call(
        _ffn_kernel,
        grid=(t // FFN_TM,),
        in_specs=[
            pl.BlockSpec((FFN_TM, D_MODEL), lambda i: (i, 0)),
            _resident((None, D_MODEL, D_FF), lambda i: (layer, 0, 0)),
            _resident((None, D_MODEL, D_FF), lambda i: (layer, 0, 0)),
            _resident((None, D_FF, D_MODEL), lambda i: (layer, 0, 0)),
            _resident((None, 1, D_MODEL), lambda i: (ln_row, 0, 0)),
            _resident((None, 1, D_MODEL), lambda i: (ln_row, 0, 0)),
        ],
        out_specs=pl.BlockSpec((FFN_TM, D_MODEL), lambda i: (i, 0)),
        out_shape=jax.ShapeDtypeStruct((t, D_MODEL), F32),
        scratch_shapes=[pltpu.VMEM((FFN_TM, D_FF), BF16)],
        compiler_params=pltpu.CompilerParams(
            dimension_semantics=("parallel",),
            vmem_limit_bytes=VMEM_LIMIT),
        name="ffn",
    )(x, wg, wu, wd, ln_g, ln_b)


def _pool_mixer(eu_ref, i, seq_len, pw_ref, ps_ref):
    half_lanes = LANES // 2
    lane = lax.broadcasted_iota(jnp.int32, (PROJ_TM, LANES), 1)
    upper = lane >= half_lanes
    t = i * PROJ_TM + lax.broadcasted_iota(jnp.int32, (PROJ_TM, LANES), 0)

    def shifted(j, c0):
        return eu_ref[pl.ds(HALO + j, PROJ_TM), c0:c0 + LANES]

    def window_sum(c0, w_small):
        hs = w_small // 2
        inner = shifted(-hs, c0)
        for j in range(-hs + 1, hs):
            inner = inner + shifted(j, c0)
        outer = shifted(-2 * hs, c0)
        for j in list(range(-2 * hs + 1, -hs)) + list(range(hs, 2 * hs)):
            outer = outer + shifted(j, c0)
        total = inner + jnp.where(upper, outer, 0.0)
        half = jnp.where(upper, 2 * hs, hs)
        cnt = jnp.minimum(t + half, seq_len) - jnp.maximum(t - half, 0)
        return total / cnt.astype(F32) - shifted(0, c0)

    p = jnp.concatenate([window_sum(0, POOL_WINDOWS[0]), window_sum(LANES, POOL_WINDOWS[2])], axis=-1)
    y = jnp.dot(p.astype(BF16), pw_ref[...], preferred_element_type=F32)
    return y * ps_ref[...]


def _in_proj_kernel(seq_len, x_ref, xp_ref, xn_ref, w_ref, pw_ref, ps_ref, cw_ref,
                    yab_ref, q_ref, k_ref, v_ref, eu_ref, ez_ref):
    n_i = seq_len // PROJ_TM
    i = pl.program_id(0) % n_i
    n_a = D_POOL + 3 * D_CONV
    c_gb, c_gc, c_h = D_POOL, D_POOL + D_CONV, D_POOL + 2 * D_CONV

    x = x_ref[...]
    xp = jnp.where(i > 0, xp_ref[...], 0.0)
    xn = jnp.where(i < n_i - 1, xn_ref[...], 0.0)
    x_ext = jnp.concatenate([xp, x, xn], axis=0).astype(BF16)
    pa = jnp.dot(x_ext, w_ref[:, 0:n_a], preferred_element_type=F32)
    eu_ref[...] = pa[:, 0:D_POOL]
    ez_ref[...] = pa[:, c_gc:c_h] * pa[:, c_h:]
    gate_b = pa[HALO:HALO + PROJ_TM, c_gb:c_gc]

    xb = x.astype(BF16)
    q_ref[...] = jnp.dot(xb, w_ref[:, n_a:n_a + D_NA],
                         preferred_element_type=F32) * (NA_HEAD_DIM ** -0.5 * LOG2E)
    k_ref[...] = jnp.dot(xb, w_ref[:, n_a + D_NA:n_a + 2 * D_NA],
                         preferred_element_type=F32).astype(BF16)
    v_ref[...] = jnp.dot(xb, w_ref[:, n_a + 2 * D_NA:n_a + 3 * D_NA],
                         preferred_element_type=F32).astype(BF16)

    yab_ref[:, 0:D_POOL] = _pool_mixer(eu_ref, i, seq_len, pw_ref, ps_ref).astype(BF16)
    conv = (cw_ref[0:1, :] * ez_ref[pl.ds(HALO - 1, PROJ_TM), :]
            + cw_ref[1:2, :] * ez_ref[pl.ds(HALO, PROJ_TM), :]
            + cw_ref[2:3, :] * ez_ref[pl.ds(HALO + 1, PROJ_TM), :])
    yab_ref[:, D_POOL:] = (gate_b * conv).astype(BF16)


def _in_proj(x, w_in, pool_w, pool_scale, conv_w, layer, seq_len):
    t = x.shape[0]
    assert seq_len % PROJ_TM == 0
    halo_blocks = PROJ_TM // HALO
    row = lambda i: (i, 0)
    halo_prev = lambda i: (jnp.maximum(i * halo_blocks - 1, 0), 0)
    halo_next = lambda i: (jnp.minimum((i + 1) * halo_blocks, t // HALO - 1), 0)
    return pl.pallas_call(
        functools.partial(_in_proj_kernel, seq_len),
        grid=(t // PROJ_TM,),
        in_specs=[
            pl.BlockSpec((PROJ_TM, D_MODEL), row),
            pl.BlockSpec((HALO, D_MODEL), halo_prev),
            pl.BlockSpec((HALO, D_MODEL), halo_next),
            _resident((None, D_MODEL, D_IN), lambda i: (layer, 0, 0)),
            _resident((None, D_POOL, D_POOL), lambda i: (layer, 0, 0)),
            _resident((None, 1, D_POOL), lambda i: (layer, 0, 0)),
            _resident((None, 3, D_CONV), lambda i: (layer, 0, 0)),
        ],
        out_specs=[
            pl.BlockSpec((PROJ_TM, D_POOL + D_CONV), row),
            pl.BlockSpec((PROJ_TM, D_NA), row),
            pl.BlockSpec((PROJ_TM, D_NA), row),
            pl.BlockSpec((PROJ_TM, D_NA), row),
        ],
        out_shape=[
            jax.ShapeDtypeStruct((t, D_POOL + D_CONV), BF16),
            jax.ShapeDtypeStruct((t, D_NA), F32),
            jax.ShapeDtypeStruct((t, D_NA), BF16),
            jax.ShapeDtypeStruct((t, D_NA), BF16),
        ],
        scratch_shapes=[
            pltpu.VMEM((PROJ_TM + 2 * HALO, D_POOL), F32),
            pltpu.VMEM((PROJ_TM + 2 * HALO, D_CONV), F32),
        ],
        compiler_params=pltpu.CompilerParams(
            dimension_semantics=("parallel",),
            vmem_limit_bytes=VMEM_LIMIT),
        name="in_proj",
    )(x, x, x, w_in, pool_w, pool_scale, conv_w)


def _mixer_kernel(yab_ref, q_ref, kp_ref, kc_ref, kn_ref, vp_ref, vc_ref, vn_ref,
                  bias0_ref, bias1_ref, bias2_ref, x_ref, wo_ref, g_ref, b_ref,
                  o_ref, yc_ref):
    half_rows = NA_ROWS // 2
    slab = HEADS_PER_GROUP * NA_HEAD_DIM
    head_of_lane = lax.broadcasted_iota(jnp.int32, (1, slab), 1) // NA_HEAD_DIM
    sels = [head_of_lane == hh for hh in range(HEADS_PER_GROUP)]
    for (qc0, mc, kc0), bias_ref in zip(COL_BLOCKS, (bias0_ref, bias1_ref, bias2_ref)):
        m = ROWS_PER_BLOCK * mc
        for hg in range(NA_HEADS // HEADS_PER_GROUP):
            l0 = hg * slab
            kwin = jnp.concatenate([kp_ref[half_rows:, kc0:kc0 + KEY_COLS, l0:l0 + slab],
                                    kc_ref[:, kc0:kc0 + KEY_COLS, l0:l0 + slab],
                                    kn_ref[0:half_rows, kc0:kc0 + KEY_COLS, l0:l0 + slab]],
                                   axis=0).reshape(N_KEYS, slab)
            vwin = jnp.concatenate([vp_ref[half_rows:, kc0:kc0 + KEY_COLS, l0:l0 + slab],
                                    vc_ref[:, kc0:kc0 + KEY_COLS, l0:l0 + slab],
                                    vn_ref[0:half_rows, kc0:kc0 + KEY_COLS, l0:l0 + slab]],
                                   axis=0).reshape(N_KEYS, slab)
            qt = q_ref[:, qc0:qc0 + mc, l0:l0 + slab].reshape(m, slab)
            q4 = jnp.concatenate([jnp.where(sel, qt, 0.0) for sel in sels], axis=0).astype(BF16)
            s = lax.dot_general(q4, kwin, (((1,), (1,)), ((), ())), preferred_element_type=F32)
            s = s + bias_ref[hg * HEADS_PER_GROUP * m:(hg + 1) * HEADS_PER_GROUP * m, :]
            p = jnp.exp2(s - jnp.max(s, axis=-1, keepdims=True))
            inv = 1.0 / jnp.sum(p, axis=-1, keepdims=True)
            o4 = jnp.dot(p.astype(BF16), vwin, preferred_element_type=F32) * inv
            acc = o4[(HEADS_PER_GROUP - 1) * m:]
            for hh in range(HEADS_PER_GROUP - 2, -1, -1):
                acc = jnp.where(sels[hh], o4[hh * m:(hh + 1) * m], acc)
            yc_ref[:, qc0:qc0 + mc, l0:l0 + slab] = acc.reshape(ROWS_PER_BLOCK, mc, slab)

    y = jnp.concatenate([yab_ref[...], yc_ref[...].reshape(MIX_TM, D_NA).astype(BF16)], axis=-1)
    mixed = jnp.dot(y, wo_ref[...], preferred_element_type=F32)
    o_ref[...] = _layer_norm(ALPHA * x_ref[...] + mixed, g_ref[...], b_ref[...])


def _row_lo(ql, variant):
    half_rows = NA_ROWS // 2
    if variant == 0:
        return np.maximum(ql - half_rows, 0) + half_rows
    if variant == 2:
        return np.minimum(ql, half_rows)
    return ql


def _natten_bias_tables(rpb):
    n_l, n_h = rpb.shape[:2]
    half_rows = NA_ROWS // 2
    pad_c = GRID_W - NA_COLS
    n_pr = 2 * NA_ROWS - 1 + 2 * half_rows
    rp = jnp.pad(rpb, ((0, 0), (0, 0), (half_rows, half_rows), (pad_c, pad_c)), mode="edge")
    ql = np.arange(ROWS_PER_BLOCK)[:, None]
    kl = np.arange(KEY_ROWS)[None, :]
    tables = []
    for (qc0, mc, kc0) in COL_BLOCKS:
        starts = [kc0 - qc + (NA_COLS - 1) + pad_c for qc in range(qc0, qc0 + mc)]
        cols = jnp.stack([rp[..., s:s + KEY_COLS] for s in starts], axis=2)
        cols = cols.reshape(n_l, n_h, mc, n_pr * KEY_COLS)
        rows = jnp.stack([cols[..., (NA_ROWS - 1 - r) * KEY_COLS:(NA_ROWS - 1 - r) * KEY_COLS + N_KEYS]
                          for r in range(ROWS_PER_BLOCK)], axis=2)
        qc = qc0 + np.arange(mc)
        kc = kc0 + np.arange(KEY_COLS)
        cs = np.clip(qc - NA_COLS // 2, 0, GRID_W - NA_COLS)
        valid = (kc[None, :] >= cs[:, None]) & (kc[None, :] < cs[:, None] + NA_COLS)
        shape = (ROWS_PER_BLOCK, mc, KEY_ROWS, KEY_COLS)
        masks = []
        for variant in range(3):
            lo = _row_lo(ql, variant)
            row_valid = (kl >= lo) & (kl < lo + NA_ROWS)
            ok = np.broadcast_to(valid[None, :, None, :] & row_valid[:, None, :, None], shape)
            masks.append(np.where(ok, 0.0, NEG_INF).reshape(ROWS_PER_BLOCK, mc, N_KEYS).astype(np.float32))
        mask = np.stack(masks)[None, :, None]
        tab = rows[:, None] * LOG2E + mask
        tables.append(tab.reshape(n_l, 3, n_h * ROWS_PER_BLOCK * mc, N_KEYS))
    return tables


def _mixer(yab, q, k, v, biases, x, w_out, ln_g, ln_b, layer, batch, seq_len):
    t = x.shape[0]
    n_i = seq_len // MIX_TM
    assert seq_len % MIX_TM == 0 and n_i >= 2
    rows_total = t // GRID_W
    q3 = q.reshape(rows_total, GRID_W, D_NA)
    k3 = k.reshape(rows_total, GRID_W, D_NA)
    v3 = v.reshape(rows_total, GRID_W, D_NA)
    ln_row = layer * 3 + 1

    blk = lambda b, i: (b * n_i + i, 0)
    blk3 = lambda b, i: (b * n_i + i, 0, 0)
    prev3 = lambda b, i: (b * n_i + jnp.maximum(i - 1, 0), 0, 0)
    next3 = lambda b, i: (b * n_i + jnp.minimum(i + 1, n_i - 1), 0, 0)
    kv_spec = lambda im: pl.BlockSpec((ROWS_PER_BLOCK, GRID_W, D_NA), im)
    variant = lambda b, i: (layer, (i > 0).astype(jnp.int32) + (i == n_i - 1).astype(jnp.int32), 0, 0)
    bias_specs = [pl.BlockSpec((None, None, NA_HEADS * ROWS_PER_BLOCK * mc, N_KEYS), variant)
                  for (_, mc, _) in COL_BLOCKS]

    return pl.pallas_call(
        _mixer_kernel,
        grid=(batch, n_i),
        in_specs=[
            pl.BlockSpec((MIX_TM, D_POOL + D_CONV), blk),
            kv_spec(blk3),
            kv_spec(prev3), kv_spec(blk3), kv_spec(next3),
            kv_spec(prev3), kv_spec(blk3), kv_spec(next3),
            *bias_specs,
            pl.BlockSpec((MIX_TM, D_MODEL), blk),
            _resident((None, D_MIX, D_MODEL), lambda b, i: (layer, 0, 0)),
            _resident((None, 1, D_MODEL), lambda b, i: (ln_row, 0, 0)),
            _resident((None, 1, D_MODEL), lambda b, i: (ln_row, 0, 0)),
        ],
        out_specs=pl.BlockSpec((MIX_TM, D_MODEL), blk),
        out_shape=jax.ShapeDtypeStruct((t, D_MODEL), F32),
        scratch_shapes=[pltpu.VMEM((ROWS_PER_BLOCK, GRID_W, D_NA), F32)],
        compiler_params=pltpu.CompilerParams(
            dimension_semantics=("parallel", "arbitrary"),
            vmem_limit_bytes=VMEM_LIMIT),
        name="mixer",
    )(yab, q3, k3, k3, k3, v3, v3, v3, *biases, x, w_out, ln_g, ln_b)


def kernel(x, ffn1_w_gate, ffn1_w_up, ffn1_w_down, ffn2_w_gate, ffn2_w_up, ffn2_w_down,
           w_in, pool_w, pool_scale, conv_w, rpb, w_out, ln_g, ln_b):
    batch, seq_len, d = x.shape
    depth = w_in.shape[0]
    assert d == D_MODEL and depth == DEPTH and seq_len % GRID_W == 0
    t = batch * seq_len

    w1g, w1u, w1d = ffn1_w_gate.astype(BF16), ffn1_w_up.astype(BF16), ffn1_w_down.astype(BF16)
    w2g, w2u, w2d = ffn2_w_gate.astype(BF16), ffn2_w_up.astype(BF16), ffn2_w_down.astype(BF16)
    w_in_b = w_in.astype(BF16)
    w_out_b = w_out.astype(BF16)
    n_g = len(POOL_WINDOWS)
    pool_bd = (jnp.eye(n_g, dtype=F32)[None, :, None, :, None] * pool_w[:, :, :, None, :]
               ).reshape(depth, D_POOL, D_POOL).astype(BF16)
    pool_sc = pool_scale.reshape(depth, 1, D_POOL)
    biases = _natten_bias_tables(rpb)
    g3 = ln_g.reshape(depth * 3, 1, D_MODEL)
    b3 = ln_b.reshape(depth * 3, 1, D_MODEL)

    h = x.reshape(t, D_MODEL)
    for l in range(depth):
        h = _ffn(h, w1g, w1u, w1d, g3, b3, l, 0)
        yab, q, k, v = _in_proj(h, w_in_b, pool_bd, pool_sc, conv_w, l, seq_len)
        h = _mixer(yab, q, k, v, biases, h, w_out_b, g3, b3, l, batch, seq_len)
        h = _ffn(h, w2g, w2u, w2d, g3, b3, l, 2)
    return h.reshape(batch, seq_len, D_MODEL)
```
